```python
import math
import jax, jax.numpy as jnp
from jax import lax
import numpy as np

D_MODEL = 1024
BATCH = 1
SEQ = 16384
DEPTH = 2

N_MIXERS = 2
RET_HEADS = 4
RET_QK_DIM = D_MODEL // RET_HEADS
RET_V_DIM = 2 * D_MODEL // RET_HEADS
RET_CHUNK = 128
ROPE_BASE = 10000.0
MLSTM_HEADS = 4
MLSTM_INNER = 2 * D_MODEL
MLSTM_QK_DIM = D_MODEL // MLSTM_HEADS
MLSTM_V_DIM = MLSTM_INNER // MLSTM_HEADS
MLSTM_CONV = 4
MLSTM_CHUNK = 128
D_FF = 2816
FFN_CONV = 3
EPS = 1e-6

N_RET = (DEPTH + 1) // 2
N_MLSTM = DEPTH // 2
RET_IN = 2 * RET_HEADS * RET_QK_DIM + 2 * RET_HEADS * RET_V_DIM
MLSTM_IN = 2 * MLSTM_HEADS * MLSTM_QK_DIM + 2 * MLSTM_INNER + 2 * MLSTM_HEADS

kernel_name = "hybrid_retention_mlstm_convffn"


def rmsnorm(x, g):
    xf = x.astype(jnp.float32)
    xn = xf * lax.rsqrt(jnp.mean(xf * xf, axis=-1, keepdims=True) + EPS)
    return xn.astype(x.dtype) * g


def head_groupnorm(y, g):
    yf = y.astype(jnp.float32)
    mu = jnp.mean(yf, axis=-1, keepdims=True)
    var = jnp.mean((yf - mu) ** 2, axis=-1, keepdims=True)
    yn = (yf - mu) * lax.rsqrt(var + EPS)
    Bt, S = y.shape[0], y.shape[1]
    return yn.reshape(Bt, S, -1) * g


def causal_dwconv(x, w, b):
    K = w.shape[0]
    S = x.shape[1]
    xp = jnp.pad(x, ((0, 0), (K - 1, 0), (0, 0)))
    out = b + xp[:, 0:S] * w[0]
    for j in range(1, K):
        out = out + xp[:, j:j + S] * w[j]
    return out


def rope(t, positions):
    d = t.shape[-1]
    inv_freq = ROPE_BASE ** (-jnp.arange(0, d, 2, dtype=jnp.float32) / d)
    ang = positions.astype(jnp.float32)[..., None] * inv_freq
    cos = jnp.cos(ang)[:, :, None, :].astype(t.dtype)
    sin = jnp.sin(ang)[:, :, None, :].astype(t.dtype)
    t1, t2 = t[..., : d // 2], t[..., d // 2:]
    return jnp.concatenate([t1 * cos - t2 * sin, t1 * sin + t2 * cos], axis=-1)


def to_chunks(t, C):
    Bt, S, H, d = t.shape
    return t.reshape(Bt, S // C, C, H, d).transpose(1, 0, 3, 2, 4)


def gates_to_chunks(t, C):
    Bt, S, H = t.shape
    return t.reshape(Bt, S // C, C, H).transpose(1, 0, 3, 2)


def from_chunks(t):
    nC, Bt, H, C, d = t.shape
    return t.transpose(1, 0, 3, 2, 4).reshape(Bt, nC * C, H, d)


def retention_mixer(h, positions, w_in, gn_g, w_out):
    Bt, S, _ = h.shape
    H, dk, dv, C = RET_HEADS, RET_QK_DIM, RET_V_DIM, RET_CHUNK
    proj = h @ w_in
    q, k, v, g = jnp.split(proj, [H * dk, 2 * H * dk, 2 * H * dk + H * dv], axis=-1)
    q = rope(q.reshape(Bt, S, H, dk), positions)
    k = rope(k.reshape(Bt, S, H, dk), positions) * (dk ** -0.5)
    v = v.reshape(Bt, S, H, dv)
    log_gamma = jnp.log(1.0 - 2.0 ** (-5.0 - jnp.arange(H, dtype=jnp.float32)))
    idx = jnp.arange(C, dtype=jnp.float32)
    rel = idx[:, None] - idx[None, :]
    decay = jnp.where(rel >= 0, jnp.exp(rel[None] * log_gamma[:, None, None]), 0.0)
    xi = jnp.exp((idx + 1.0)[None] * log_gamma[:, None])
    zeta = jnp.exp((C - 1.0 - idx)[None] * log_gamma[:, None])
    gamma_C = jnp.exp(C * log_gamma)

    def step(R, inp):
        qc, kc, vc = inp
        s = jnp.einsum('bhid,bhjd->bhij', qc, kc) * decay
        intra = jnp.einsum('bhij,bhjv->bhiv', s, vc)
        cross = jnp.einsum('bhid,bhdv->bhiv', qc, R) * xi[:, :, None]
        R_new = R * gamma_C[:, None, None] + jnp.einsum('bhjd,bhjv->bhdv', kc * zeta[:, :, None], vc)
        return R_new, intra + cross

    R0 = jnp.zeros((Bt, H, dk, dv), jnp.float32)
    _, ys = lax.scan(step, R0, (to_chunks(q, C), to_chunks(k, C), to_chunks(v, C)))
    y = head_groupnorm(from_chunks(ys), gn_g).astype(h.dtype)
    return (jax.nn.silu(g) * y) @ w_out


def mlstm_mixer(h, w_in, b_gate, conv_w, conv_b, gn_g, w_out):
    Bt, S, _ = h.shape
    H, dk, dv, C = MLSTM_HEADS, MLSTM_QK_DIM, MLSTM_V_DIM, MLSTM_CHUNK
    proj = h @ w_in
    qk, v, o_pre, gate_pre = jnp.split(
        proj, [2 * H * dk, 2 * H * dk + MLSTM_INNER, 2 * H * dk + 2 * MLSTM_INNER], axis=-1)
    qk = jax.nn.silu(causal_dwconv(qk, conv_w, conv_b))
    q, k = jnp.split(qk, 2, axis=-1)
    q = q.reshape(Bt, S, H, dk)
    k = k.reshape(Bt, S, H, dk) * (dk ** -0.5)
    v = v.reshape(Bt, S, H, dv)
    gate_pre = gate_pre.astype(jnp.float32) + b_gate.astype(jnp.float32)
    log_i = gate_pre[..., :H]
    log_f = jax.nn.log_sigmoid(gate_pre[..., H:])
    g_cum = lax.cumsum(gates_to_chunks(log_f, C), axis=3)
    ic_all = gates_to_chunks(log_i, C)
    causal = jnp.tril(jnp.ones((C, C), dtype=bool))

    def step(carry, inp):
        Cm, n, m = carry
        qc, kc, vc, gc, ic = inp
        a = gc + m[..., None]
        dlog = gc[..., :, None] - gc[..., None, :] + ic[..., None, :]
        dlog = jnp.where(causal, dlog, -jnp.inf)
        m_row = jnp.maximum(a, jnp.max(dlog, axis=-1))
        w_intra = jnp.exp(dlog - m_row[..., None])
        w_inter = jnp.exp(a - m_row)
        s = jnp.einsum('bhid,bhjd->bhij', qc, kc) * w_intra
        num = jnp.einsum('bhij,bhjv->bhiv', s, vc) + w_inter[..., None] * jnp.einsum('bhid,bhdv->bhiv', qc, Cm)
        den = jnp.sum(s, axis=-1) + w_inter * jnp.einsum('bhid,bhd->bhi', qc, n)
        h_til = num / jnp.maximum(jnp.abs(den), jnp.exp(-m_row))[..., None]
        gL = gc[..., -1]
        dec_end = gL[..., None] - gc + ic
        m_new = jnp.maximum(gL + m, jnp.max(dec_end, axis=-1))
        w_state = jnp.exp(dec_end - m_new[..., None])
        scale_prev = jnp.exp(gL + m - m_new)
        Cm_new = scale_prev[..., None, None] * Cm + jnp.einsum('bhj,bhjd,bhjv->bhdv', w_state, kc, vc)
        n_new = scale_prev[..., None] * n + jnp.einsum('bhj,bhjd->bhd', w_state, kc)
        return (Cm_new, n_new, m_new), h_til

    carry0 = (jnp.zeros((Bt, H, dk, dv), jnp.float32),
              jnp.zeros((Bt, H, dk), jnp.float32),
              jnp.zeros((Bt, H), jnp.float32))
    _, hs = lax.scan(step, carry0, (to_chunks(q, C), to_chunks(k, C), to_chunks(v, C), g_cum, ic_all))
    h_til = from_chunks(hs)
    o = jax.nn.sigmoid(o_pre.astype(jnp.float32)).reshape(Bt, S, H, dv)
    y = head_groupnorm(o * h_til, gn_g).astype(h.dtype)
    return y @ w_out


def conv_ffn(h, w_up, conv_w, conv_b, w_down):
    u = causal_dwconv(h @ w_up, conv_w, conv_b)
    a, b = jnp.split(u, 2, axis=-1)
    return (jax.nn.silu(a) * b) @ w_down


def setup_inputs(seed: int = 0) -> dict:
    key = jax.random.key(seed)
    ks = jax.random.split(key, 24)
    f32 = jnp.float32
    nrm = lambda k, shape, s: jax.random.normal(k, shape, f32) * s
    D = D_MODEL
    x = nrm(ks[0], (BATCH, SEQ, D), 1.0)
    c = nrm(ks[1], (BATCH, D), 1.0)
    positions = jnp.broadcast_to(jnp.arange(SEQ, dtype=jnp.int32), (BATCH, SEQ))
    ada_w = nrm(ks[2], (DEPTH, D, 6 * D), 0.5 * D ** -0.5)
    ada_b = nrm(ks[3], (DEPTH, 6 * D), 0.02)
    norm_tok_g = 1.0 + nrm(ks[4], (DEPTH, D), 0.02)
    norm_ffn_g = 1.0 + nrm(ks[5], (DEPTH, D), 0.02)
    ret_w_in = nrm(ks[6], (N_RET, D, RET_IN), D ** -0.5)
    ret_gn_g = 1.0 + nrm(ks[7], (N_RET, RET_HEADS * RET_V_DIM), 0.02)
    ret_w_out = nrm(ks[8], (N_RET, RET_HEADS * RET_V_DIM, D), (RET_HEADS * RET_V_DIM) ** -0.5)
    ml_w_in = nrm(ks[9], (N_MLSTM, D, MLSTM_IN), D ** -0.5)
    i_bias = nrm(ks[10], (N_MLSTM, MLSTM_HEADS), 0.1)
    f_bias = jnp.linspace(3.0, 6.0, MLSTM_HEADS, dtype=f32)[None] + nrm(ks[11], (N_MLSTM, MLSTM_HEADS), 0.1)
    ml_b_gate = jnp.concatenate([i_bias, f_bias], axis=-1)
    qk_width = 2 * MLSTM_HEADS * MLSTM_QK_DIM
    ml_conv_w = nrm(ks[12], (N_MLSTM, MLSTM_CONV, qk_width), MLSTM_CONV ** -0.5)
    ml_conv_b = nrm(ks[13], (N_MLSTM, qk_width), 0.02)
    ml_gn_g = 1.0 + nrm(ks[14], (N_MLSTM, MLSTM_INNER), 0.02)
    ml_w_out = nrm(ks[15], (N_MLSTM, MLSTM_INNER, D), MLSTM_INNER ** -0.5)
    ffn_w_up = nrm(ks[16], (DEPTH, D, 2 * D_FF), D ** -0.5)
    ffn_conv_w = nrm(ks[17], (DEPTH, FFN_CONV, 2 * D_FF), FFN_CONV ** -0.5)
    ffn_conv_b = nrm(ks[18], (DEPTH, 2 * D_FF), 0.02)
    ffn_w_down = nrm(ks[19], (DEPTH, D_FF, D), D_FF ** -0.5)
    final_g = 1.0 + nrm(ks[20], (D,), 0.02)
    return {"x": x, "c": c, "positions": positions,
            "ada_w": ada_w, "ada_b": ada_b, "norm_tok_g": norm_tok_g, "norm_ffn_g": norm_ffn_g,
            "ret_w_in": ret_w_in, "ret_gn_g": ret_gn_g, "ret_w_out": ret_w_out,
            "ml_w_in": ml_w_in, "ml_b_gate": ml_b_gate, "ml_conv_w": ml_conv_w, "ml_conv_b": ml_conv_b,
            "ml_gn_g": ml_gn_g, "ml_w_out": ml_w_out,
            "ffn_w_up": ffn_w_up, "ffn_conv_w": ffn_conv_w, "ffn_conv_b": ffn_conv_b, "ffn_w_down": ffn_w_down,
            "final_g": final_g}


def reference(x, c, positions, ada_w, ada_b, norm_tok_g, norm_ffn_g,
              ret_w_in, ret_gn_g, ret_w_out,
              ml_w_in, ml_b_gate, ml_conv_w, ml_conv_b, ml_gn_g, ml_w_out,
              ffn_w_up, ffn_conv_w, ffn_conv_b, ffn_w_down, final_g):
    c_act = jax.nn.silu(c)
    for i in range(DEPTH):
        mod = c_act @ ada_w[i] + ada_b[i]
        sh_t, sc_t, gt_t, sh_f, sc_f, gt_f = [m[:, None, :] for m in jnp.split(mod, 6, axis=-1)]
        h = rmsnorm(x, norm_tok_g[i]) * (1.0 + sc_t) + sh_t
        j = i // N_MIXERS
        if i % N_MIXERS == 0:
            y = retention_mixer(h, positions, ret_w_in[j], ret_gn_g[j], ret_w_out[j])
        else:
            y = mlstm_mixer(h, ml_w_in[j], ml_b_gate[j], ml_conv_w[j], ml_conv_b[j], ml_gn_g[j], ml_w_out[j])
        x = x + gt_t * y
        h = rmsnorm(x, norm_ffn_g[i]) * (1.0 + sc_f) + sh_f
        x = x + gt_f * conv_ffn(h, ffn_w_up[i], ffn_conv_w[i], ffn_conv_b[i], ffn_w_down[i])
    return rmsnorm(x, final_g)
```

```python
import functools

import jax
import jax.numpy as jnp
from jax import lax
from jax.experimental import pallas as pl
from jax.experimental.pallas import tpu as pltpu

F32 = jnp.float32
BF16 = jnp.bfloat16

D_MODEL = 1024
DEPTH = 2
HEADS = 4
QK_DIM = 256
V_DIM = 512
MLSTM_CONV = 4
D_FF = 2816
FFN_CONV = 3
ROPE_BASE = 10000.0
EPS = 1e-6

V7X_VMEM_LIMIT_BYTES = 56 * 1024 * 1024
SUBLANES = 8
LANES = 128

ROW_BLOCK = 512
CHUNK = 256
FFN_COLS = 256


def _resident(shape):
    nd = len(shape)
    return pl.BlockSpec(shape, lambda i: (0,) * nd, pipeline_mode=pl.Buffered(1))


def _params():
    return pltpu.CompilerParams(dimension_semantics=("arbitrary",),
                                vmem_limit_bytes=V7X_VMEM_LIMIT_BYTES)


def _dot(a, b):
    return jnp.dot(a, b, preferred_element_type=F32)


def _dot_nt(a, b):
    return lax.dot_general(a, b, (((1,), (1,)), ((), ())), preferred_element_type=F32)


def _dot_tn(a, b):
    return lax.dot_general(a, b, (((0,), (0,)), ((), ())), preferred_element_type=F32)


def _sigmoid(x):
    return 1.0 / (1.0 + jnp.exp(-x))


def _norm_mod(x, vec_ref):
    ms = jnp.mean(x * x, axis=-1, keepdims=True)
    xn = x * lax.rsqrt(ms + EPS)
    return xn * vec_ref[0:1, :] * (1.0 + vec_ref[1:2, :]) + vec_ref[2:3, :]


def _head_norm(y, gain):
    mu = jnp.mean(y, axis=-1, keepdims=True)
    yc = y - mu
    var = jnp.mean(yc * yc, axis=-1, keepdims=True)
    return yc * lax.rsqrt(var + EPS) * gain


def _mod_kernel(c_ref, w_ref, b_ref, o_ref):
    c = c_ref[...]
    ca = c * _sigmoid(c)
    o_ref[0] = jnp.sum(w_ref[0] * ca, axis=0, keepdims=True) + b_ref[0]


def _modulation(c, ada_w, ada_b):
    depth, d, n = ada_w.shape
    tn = 1536
    return pl.pallas_call(
        _mod_kernel,
        out_shape=jax.ShapeDtypeStruct((depth, 1, n), F32),
        grid=(depth, n // tn),
        in_specs=[pl.BlockSpec((d, 1), lambda l, j: (0, 0)),
                  pl.BlockSpec((1, d, tn), lambda l, j: (l, 0, j)),
                  pl.BlockSpec((1, 1, tn), lambda l, j: (l, 0, j))],
        out_specs=pl.BlockSpec((1, 1, tn), lambda l, j: (l, 0, j)),
        compiler_params=pltpu.CompilerParams(dimension_semantics=("arbitrary", "arbitrary"),
                                             vmem_limit_bytes=V7X_VMEM_LIMIT_BYTES),
        name="mod",
    )(c.reshape(d, 1), ada_w, ada_b.reshape(depth, 1, n))


def _rope_kernel(pos_ref, inv_ref, cos_ref, sin_ref):
    ang = pos_ref[...].astype(F32) * inv_ref[...]
    c = jnp.cos(ang)
    s = jnp.sin(ang)
    cos_ref[...] = jnp.concatenate([c, c], axis=1)
    sin_ref[...] = jnp.concatenate([-s, s], axis=1)


def _rope_tables(positions, seq):
    half = QK_DIM // 2
    inv_freq = ROPE_BASE ** (-jnp.arange(0, QK_DIM, 2, dtype=F32) / QK_DIM)
    ts = min(2048, seq)
    return pl.pallas_call(
        _rope_kernel,
        out_shape=(jax.ShapeDtypeStruct((seq, QK_DIM), F32),) * 2,
        grid=(seq // ts,),
        in_specs=[pl.BlockSpec((ts, 1), lambda i: (i, 0)),
                  pl.BlockSpec((1, half), lambda i: (0, 0))],
        out_specs=(pl.BlockSpec((ts, QK_DIM), lambda i: (i, 0)),) * 2,
        compiler_params=_params(),
        name="rope",
    )(positions.reshape(seq, 1), inv_freq.reshape(1, half))


def _ret_kernel(gam_ref, x_ref, vec_ref, cos_ref, sin_ref, wq_ref, wk_ref, wv_ref, wg_ref,
                decay_ref, xi_ref, zeta_ref, gn_ref, wo_ref,
                o_ref,
                r_s, h_s, q_s, k_s, v_s, g_s, z_s, *, tm, chunk):
    @pl.when(pl.program_id(0) == 0)
    def _():
        r_s[...] = jnp.zeros_like(r_s)

    x = x_ref[...]
    h_s[...] = _norm_mod(x, vec_ref).astype(BF16)

    def rope(t):
        return t * cos_ref[...] + pltpu.roll(t, QK_DIM // 2, 1) * sin_ref[...]

    def project(h, carry):
        hb = h_s[...]
        q_s[h] = rope(_dot(hb, wq_ref[h])).astype(BF16)
        k_s[h] = (rope(_dot(hb, wk_ref[h])) * (QK_DIM ** -0.5)).astype(BF16)
        v_s[h] = _dot(hb, wv_ref[h]).astype(BF16)
        g = _dot(hb, wg_ref[h])
        g_s[h] = (g * _sigmoid(g)).astype(BF16)
        return carry

    lax.fori_loop(0, HEADS, project, 0)

    for c in range(tm // chunk):
        rows = pl.ds(c * chunk, chunk)
        for h in range(HEADS):
            qc = q_s[h, rows, :]
            kc = k_s[h, rows, :]
            vc = v_s[h, rows, :]
            s = _dot_nt(qc, kc) * decay_ref[h]
            y = _dot(s.astype(BF16), vc) + _dot(qc, r_s[h].astype(BF16)) * xi_ref[h]
            kz = (kc.astype(F32) * zeta_ref[h]).astype(BF16)
            r_s[h] = r_s[h] * gam_ref[h] + _dot_tn(kz, vc)
            z = _head_norm(y, gn_ref[h]) * g_s[h, rows, :].astype(F32)
            z_s[h, rows, :] = z.astype(BF16)

    out = _dot(z_s[0], wo_ref[0])
    for h in range(1, HEADS):
        out = out + _dot(z_s[h], wo_ref[h])
    o_ref[...] = x + vec_ref[3:4, :] * out


def _retention_layer(x, vecs, cos2, sin2, w_in, gn_g, w_out, *, tm=ROW_BLOCK, chunk=CHUNK):
    seq, d = x.shape
    hq = HEADS * QK_DIM
    hv = HEADS * V_DIM

    def heads(w, width):
        return w.reshape(d, HEADS, width).transpose(1, 0, 2).astype(BF16)

    wq = heads(w_in[:, :hq], QK_DIM)
    wk = heads(w_in[:, hq:2 * hq], QK_DIM)
    wv = heads(w_in[:, 2 * hq:2 * hq + hv], V_DIM)
    wg = heads(w_in[:, 2 * hq + hv:], V_DIM)
    wo = w_out.reshape(HEADS, V_DIM, d).astype(BF16)
    gn = gn_g.reshape(HEADS, 1, V_DIM)

    log_gamma = jnp.log(1.0 - 2.0 ** (-5.0 - jnp.arange(HEADS, dtype=F32)))
    idx = jnp.arange(chunk, dtype=F32)
    rel = idx[:, None] - idx[None, :]
    decay = jnp.where(rel >= 0, jnp.exp(rel[None] * log_gamma[:, None, None]), 0.0)
    xi = jnp.exp((idx + 1.0)[None] * log_gamma[:, None])
    zeta = jnp.exp((chunk - 1.0 - idx)[None] * log_gamma[:, None])
    gamma_c = jnp.exp(chunk * log_gamma)
    xi_b = jnp.broadcast_to(xi[:, :, None], (HEADS, chunk, V_DIM))
    zeta_b = jnp.broadcast_to(zeta[:, :, None], (HEADS, chunk, QK_DIM))

    row = lambda w: pl.BlockSpec((tm, w), lambda i: (i, 0))
    return pl.pallas_call(
        functools.partial(_ret_kernel, tm=tm, chunk=chunk),
        out_shape=jax.ShapeDtypeStruct((seq, d), F32),
        grid=(seq // tm,),
        in_specs=[pl.BlockSpec(memory_space=pltpu.SMEM),
                  row(d), _resident((SUBLANES, d)), row(QK_DIM), row(QK_DIM),
                  _resident(wq.shape), _resident(wk.shape), _resident(wv.shape), _resident(wg.shape),
                  _resident(decay.shape), _resident(xi_b.shape), _resident(zeta_b.shape),
                  _resident(gn.shape), _resident(wo.shape)],
        out_specs=row(d),
        scratch_shapes=[pltpu.VMEM((HEADS, QK_DIM, V_DIM), F32),
                        pltpu.VMEM((tm, d), BF16),
                        pltpu.VMEM((HEADS, tm, QK_DIM), BF16),
                        pltpu.VMEM((HEADS, tm, QK_DIM), BF16),
                        pltpu.VMEM((HEADS, tm, V_DIM), BF16),
                        pltpu.VMEM((HEADS, tm, V_DIM), BF16),
                        pltpu.VMEM((HEADS, tm, V_DIM), BF16)],
        compiler_params=_params(),
        name="ret",
    )(gamma_c, x, vecs, cos2, sin2, wq, wk, wv, wg, decay, xi_b, zeta_b, gn, wo)


def _shifted(tail, p, k):
    ext = jnp.concatenate([tail, p], axis=0)
    return pltpu.roll(ext, k, 0)[SUBLANES:, :]


def _ffn_kernel(x_ref, vec_ref, wu_ref, cw_ref, wd_ref, fin_ref,
                o_ref,
                h_s, tail_s, act_s, *, tm, nf, final):
    @pl.when(pl.program_id(0) == 0)
    def _():
        tail_s[...] = jnp.zeros_like(tail_s)

    x = x_ref[...]
    h_s[...] = _norm_mod(x, vec_ref).astype(BF16)

    def conv(col0):
        cols = pl.ds(col0, nf)
        p = _dot(h_s[...], wu_ref[:, cols])
        tail = tail_s[:, cols]
        cw = cw_ref[:, cols]
        out = cw[FFN_CONV:FFN_CONV + 1, :] + _shifted(tail, p, FFN_CONV - 1) * cw[0:1, :]
        for j in range(1, FFN_CONV - 1):
            out = out + _shifted(tail, p, FFN_CONV - 1 - j) * cw[j:j + 1, :]
        out = out + p * cw[FFN_CONV - 1:FFN_CONV, :]
        tail_s[:, cols] = p[tm - SUBLANES:, :]
        return out

    for j in range(D_FF // nf):
        a = conv(j * nf)
        b = conv(D_FF + j * nf)
        act_s[:, pl.ds(j * nf, nf)] = (a * _sigmoid(a) * b).astype(BF16)

    y = x + vec_ref[3:4, :] * _dot(act_s[...], wd_ref[...])
    if final:
        ms = jnp.mean(y * y, axis=-1, keepdims=True)
        y = y * lax.rsqrt(ms + EPS) * fin_ref[...]
    o_ref[...] = y


def _conv_ffn_layer(x, vecs, w_up, conv_w, conv_b, w_down, final_g, *, final, tm=ROW_BLOCK, nf=FFN_COLS):
    seq, d = x.shape
    k = conv_w.shape[0]
    cw = jnp.concatenate([conv_w, conv_b[None, :], jnp.zeros((SUBLANES - k - 1, 2 * D_FF), F32)], axis=0)
    row = pl.BlockSpec((tm, d), lambda i: (i, 0))
    return pl.pallas_call(
        functools.partial(_ffn_kernel, tm=tm, nf=nf, final=final),
        out_shape=jax.ShapeDtypeStruct((seq, d), F32),
        grid=(seq // tm,),
        in_specs=[row, _resident((SUBLANES, d)), _resident(w_up.shape), _resident(cw.shape),
                  _resident(w_down.shape), _resident((1, d))],
        out_specs=row,
        scratch_shapes=[pltpu.VMEM((tm, d), BF16),
                        pltpu.VMEM((SUBLANES, 2 * D_FF), F32),
                        pltpu.VMEM((tm, D_FF), BF16)],
        compiler_params=_params(),
        name="ffn_final" if final else "ffn",
    )(x, vecs, w_up.astype(BF16), cw, w_down.astype(BF16), final_g.reshape(1, d))


def _log_sigmoid(x):
    return jnp.minimum(x, 0.0) - jnp.log1p(jnp.exp(-jnp.abs(x)))


def _chunk_cumsum(x, axis, chunk):
    pos = lax.broadcasted_iota(jnp.int32, x.shape, axis) % chunk
    step = 1
    while step < chunk:
        x = x + jnp.where(pos >= step, pltpu.roll(x, step, axis), 0.0)
        step *= 2
    return x


def _mlstm_kernel(x_ref, vec_ref, wqk_ref, cw_ref, wv_ref, wo_ref, wgc_ref, wgr_ref, bgc_ref, bgr_ref,
                  neg_ref, gn_ref, wout_ref,
                  o_ref,
                  cm_s, n_s, m_s, tail_s, h_s, qk_s, v_s, og_s, z_s, ic_s, fc_s, ir_s, fr_s, *, tm, chunk):
    @pl.when(pl.program_id(0) == 0)
    def _():
        cm_s[...] = jnp.zeros_like(cm_s)
        n_s[...] = jnp.zeros_like(n_s)
        m_s[...] = jnp.zeros_like(m_s)
        tail_s[...] = jnp.zeros_like(tail_s)

    x = x_ref[...]
    h_s[...] = _norm_mod(x, vec_ref).astype(BF16)

    def project_qk(hh, carry):
        p = _dot(h_s[...], wqk_ref[hh])
        tail = tail_s[hh]
        cw = cw_ref[hh]
        out = cw[MLSTM_CONV:MLSTM_CONV + 1, :] + _shifted(tail, p, MLSTM_CONV - 1) * cw[0:1, :]
        for j in range(1, MLSTM_CONV - 1):
            out = out + _shifted(tail, p, MLSTM_CONV - 1 - j) * cw[j:j + 1, :]
        out = out + p * cw[MLSTM_CONV - 1:MLSTM_CONV, :]
        tail_s[hh] = p[tm - SUBLANES:, :]
        scale = jnp.where(hh >= HEADS, QK_DIM ** -0.5, 1.0).astype(F32)
        qk_s[hh] = (out * _sigmoid(out) * scale).astype(BF16)
        return carry

    lax.fori_loop(0, 2 * HEADS, project_qk, 0)

    def project_vo(h, carry):
        hb = h_s[...]
        v_s[h] = _dot(hb, wv_ref[h]).astype(BF16)
        og_s[h] = _sigmoid(_dot(hb, wo_ref[h])).astype(BF16)
        return carry

    lax.fori_loop(0, HEADS, project_vo, 0)

    hb = h_s[...]
    gc = _dot(hb, wgc_ref[...]) + bgc_ref[...]
    gr = _dot_nt(wgr_ref[...], hb) + bgr_ref[...]
    ic_s[...] = gc
    ir_s[...] = gr
    fc_s[...] = _chunk_cumsum(_log_sigmoid(gc), 0, chunk)
    fr_s[...] = _chunk_cumsum(_log_sigmoid(gr), 1, chunk)

    for c in range(tm // chunk):
        r0 = c * chunk
        rows = pl.ds(r0, chunk)
        for h in range(HEADS):
            f = HEADS + h
            gcol = fc_s[rows, f:f + 1]
            icol = ic_s[rows, h:h + 1]
            grow = fr_s[f:f + 1, rows]
            irow = ir_s[h:h + 1, rows]
            g_last = fc_s[r0 + chunk - 1:r0 + chunk, f:f + 1]
            m = m_s[h, 0:1, 0:1]
            brow = irow - grow
            a = gcol + m
            dlog = (gcol + brow) + neg_ref[...]
            m_row = jnp.maximum(a, jnp.max(dlog, axis=-1, keepdims=True))
            w_intra = jnp.exp(dlog - m_row)
            w_inter = jnp.exp(a - m_row)
            qc = qk_s[h, rows, :]
            kc = qk_s[f, rows, :]
            vc = v_s[h, rows, :]
            s = _dot_nt(qc, kc) * w_intra
            num = _dot(s.astype(BF16), vc) + w_inter * _dot(qc, cm_s[h].astype(BF16))
            qn = jnp.sum(qc.astype(F32) * n_s[h, 0:1, :], axis=-1, keepdims=True)
            den = jnp.sum(s, axis=-1, keepdims=True) + w_inter * qn
            h_til = num * (1.0 / jnp.maximum(jnp.abs(den), jnp.exp(-m_row)))
            m_new = jnp.maximum(g_last + m, jnp.max(g_last + brow, axis=-1, keepdims=True))
            w_state = jnp.exp((g_last + (icol - gcol)) - m_new)
            keep = jnp.exp(g_last + m - m_new)
            kw = kc.astype(F32) * w_state
            cm_s[h] = keep * cm_s[h] + _dot_tn(kw.astype(BF16), vc)
            n_new = keep * n_s[h, 0:1, :] + jnp.sum(kw, axis=0, keepdims=True)
            n_s[h] = jnp.broadcast_to(n_new, (SUBLANES, QK_DIM))
            m_s[h] = jnp.broadcast_to(m_new, (SUBLANES, LANES))
            y = og_s[h, rows, :].astype(F32) * h_til
            z_s[h, rows, :] = _head_norm(y, gn_ref[h]).astype(BF16)

    out = _dot(z_s[0], wout_ref[0])
    for h in range(1, HEADS):
        out = out + _dot(z_s[h], wout_ref[h])
    o_ref[...] = x + vec_ref[3:4, :] * out


def _mlstm_layer(x, vecs, w_in, b_gate, conv_w, conv_b, gn_g, w_out, *, tm=ROW_BLOCK, chunk=CHUNK):
    seq, d = x.shape
    hq = HEADS * QK_DIM
    hv = HEADS * V_DIM
    n_gate = 2 * HEADS

    def heads(w, n, width):
        return w.reshape(d, n, width).transpose(1, 0, 2).astype(BF16)

    wqk = heads(w_in[:, :2 * hq], 2 * HEADS, QK_DIM)
    wv = heads(w_in[:, 2 * hq:2 * hq + hv], HEADS, V_DIM)
    wo = heads(w_in[:, 2 * hq + hv:2 * hq + 2 * hv], HEADS, V_DIM)
    w_gate = w_in[:, 2 * hq + 2 * hv:]
    wgc = jnp.pad(w_gate, ((0, 0), (0, LANES - n_gate))).astype(BF16)
    wgr = jnp.pad(w_gate.T, ((0, 2 * SUBLANES - n_gate), (0, 0))).astype(BF16)
    bgc = jnp.pad(b_gate, (0, LANES - n_gate)).reshape(1, LANES)
    bgr = jnp.pad(b_gate, (0, 2 * SUBLANES - n_gate)).reshape(2 * SUBLANES, 1)
    k = conv_w.shape[0]
    cw = jnp.concatenate([conv_w, conv_b[None, :], jnp.zeros((SUBLANES - k - 1, 2 * hq), F32)], axis=0)
    cw = cw.reshape(SUBLANES, 2 * HEADS, QK_DIM).transpose(1, 0, 2)
    idx = jnp.arange(chunk)
    neg = jnp.where(idx[:, None] >= idx[None, :], 0.0, -jnp.inf).astype(F32)
    gn = gn_g.reshape(HEADS, 1, V_DIM)
    wout = w_out.reshape(HEADS, V_DIM, d).astype(BF16)

    row = pl.BlockSpec((tm, d), lambda i: (i, 0))
    operands = (x, vecs, wqk, cw, wv, wo, wgc, wgr, bgc, bgr, neg, gn, wout)
    return pl.pallas_call(
        functools.partial(_mlstm_kernel, tm=tm, chunk=chunk),
        out_shape=jax.ShapeDtypeStruct((seq, d), F32),
        grid=(seq // tm,),
        in_specs=[row, _resident((SUBLANES, d))] + [_resident(a.shape) for a in operands[2:]],
        out_specs=row,
        scratch_shapes=[pltpu.VMEM((HEADS, QK_DIM, V_DIM), F32),
                        pltpu.VMEM((HEADS, SUBLANES, QK_DIM), F32),
                        pltpu.VMEM((HEADS, SUBLANES, LANES), F32),
                        pltpu.VMEM((2 * HEADS, SUBLANES, QK_DIM), F32),
                        pltpu.VMEM((tm, d), BF16),
                        pltpu.VMEM((2 * HEADS, tm, QK_DIM), BF16),
                        pltpu.VMEM((HEADS, tm, V_DIM), BF16),
                        pltpu.VMEM((HEADS, tm, V_DIM), BF16),
                        pltpu.VMEM((HEADS, tm, V_DIM), BF16),
                        pltpu.VMEM((tm, LANES), F32),
                        pltpu.VMEM((tm, LANES), F32),
                        pltpu.VMEM((2 * SUBLANES, tm), F32),
                        pltpu.VMEM((2 * SUBLANES, tm), F32)],
        compiler_params=_params(),
        name="mlstm",
    )(*operands)


def _vecs(norm_g, mod_row, which):
    d = norm_g.shape[0]
    sh, sc, gt = (mod_row[(3 * which + j) * d:(3 * which + j + 1) * d] for j in range(3))
    return jnp.concatenate([jnp.stack([norm_g, sc, sh, gt]), jnp.zeros((SUBLANES - 4, d), F32)], axis=0)


def kernel(x, c, positions, ada_w, ada_b, norm_tok_g, norm_ffn_g, ret_w_in, ret_gn_g, ret_w_out, ml_w_in, ml_b_gate, ml_conv_w, ml_conv_b, ml_gn_g, ml_w_out, ffn_w_up, ffn_conv_w, ffn_conv_b, ffn_w_down, final_g):
    batch, seq, d = x.shape
    assert batch == 1 and d == D_MODEL and ada_w.shape[0] == DEPTH == 2
    mod = _modulation(c, ada_w, ada_b)[:, 0, :]
    cos2, sin2 = _rope_tables(positions, seq)
    h = x.reshape(seq, d)
    h = _retention_layer(h, _vecs(norm_tok_g[0], mod[0], 0), cos2, sin2,
                         ret_w_in[0], ret_gn_g[0], ret_w_out[0])
    h = _conv_ffn_layer(h, _vecs(norm_ffn_g[0], mod[0], 1), ffn_w_up[0], ffn_conv_w[0], ffn_conv_b[0],
                        ffn_w_down[0], final_g, final=False)
    h = _mlstm_layer(h, _vecs(norm_tok_g[1], mod[1], 0), ml_w_in[0], ml_b_gate[0], ml_conv_w[0],
                     ml_conv_b[0], ml_gn_g[0], ml_w_out[0])
    h = _conv_ffn_layer(h, _vecs(norm_ffn_g[1], mod[1], 1), ffn_w_up[1], ffn_conv_w[1], ffn_conv_b[1],
                        ffn_w_down[1], final_g, final=True)
    return h.reshape(batch, seq, d)
```

```python
import functools

import jax
import jax.numpy as jnp
from jax import lax
from jax.experimental import pallas as pl
from jax.experimental.pallas import tpu as pltpu

F32 = jnp.float32
BF16 = jnp.bfloat16

D_MODEL = 1024
DEPTH = 2
HEADS = 4
QK_DIM = 256
V_DIM = 512
MLSTM_CONV = 4
D_FF = 2816
FFN_CONV = 3
ROPE_BASE = 10000.0
EPS = 1e-6

V7X_VMEM_BYTES = 64 * 1024 * 1024
V7X_VMEM_LIMIT_BYTES = V7X_VMEM_BYTES - 4 * 1024 * 1024
SUBLANES = 8
LANES = 128

ROW_BLOCK = 512
CHUNK = 256
FFN_COLS = 256


def _resident(shape):
    nd = len(shape)
    return pl.BlockSpec(shape, lambda i: (0,) * nd, pipeline_mode=pl.Buffered(1))


def _params():
    return pltpu.CompilerParams(dimension_semantics=("arbitrary",),
                                vmem_limit_bytes=V7X_VMEM_LIMIT_BYTES)


def _dot(a, b):
    return jnp.dot(a, b, preferred_element_type=F32)


def _dot_nt(a, b):
    return lax.dot_general(a, b, (((1,), (1,)), ((), ())), preferred_element_type=F32)


def _dot_tn(a, b):
    return lax.dot_general(a, b, (((0,), (0,)), ((), ())), preferred_element_type=F32)


def _sigmoid(x):
    return 1.0 / (1.0 + jnp.exp(-x))


def _norm_mod(x, vec_ref):
    ms = jnp.mean(x * x, axis=-1, keepdims=True)
    xn = x * lax.rsqrt(ms + EPS)
    return xn * vec_ref[0:1, :] * (1.0 + vec_ref[1:2, :]) + vec_ref[2:3, :]


def _head_norm(y, gain):
    mu = jnp.mean(y, axis=-1, keepdims=True)
    yc = y - mu
    var = jnp.mean(yc * yc, axis=-1, keepdims=True)
    return yc * lax.rsqrt(var + EPS) * gain


def _mod_kernel(c_ref, w_ref, b_ref, o_ref):
    c = c_ref[...]
    ca = c * _sigmoid(c)
    o_ref[0] = jnp.sum(w_ref[0] * ca, axis=0, keepdims=True) + b_ref[0]


def _modulation(c, ada_w, ada_b):
    depth, d, n = ada_w.shape
    tn = 1536
    return pl.pallas_call(
        _mod_kernel,
        out_shape=jax.ShapeDtypeStruct((depth, 1, n), F32),
        grid=(depth, n // tn),
        in_specs=[pl.BlockSpec((d, 1), lambda l, j: (0, 0)),
                  pl.BlockSpec((1, d, tn), lambda l, j: (l, 0, j)),
                  pl.BlockSpec((1, 1, tn), lambda l, j: (l, 0, j))],
        out_specs=pl.BlockSpec((1, 1, tn), lambda l, j: (l, 0, j)),
        compiler_params=pltpu.CompilerParams(dimension_semantics=("arbitrary", "arbitrary"),
                                             vmem_limit_bytes=V7X_VMEM_LIMIT_BYTES),
        name="mod",
    )(c.reshape(d, 1), ada_w, ada_b.reshape(depth, 1, n))


def _rope_kernel(pos_ref, inv_ref, cos_ref, sin_ref):
    ang = pos_ref[...].astype(F32) * inv_ref[...]
    c = jnp.cos(ang)
    s = jnp.sin(ang)
    cos_ref[...] = jnp.concatenate([c, c], axis=1)
    sin_ref[...] = jnp.concatenate([-s, s], axis=1)


def _rope_tables(positions, seq):
    half = QK_DIM // 2
    inv_freq = ROPE_BASE ** (-jnp.arange(0, QK_DIM, 2, dtype=F32) / QK_DIM)
    ts = min(2048, seq)
    return pl.pallas_call(
        _rope_kernel,
        out_shape=(jax.ShapeDtypeStruct((seq, QK_DIM), F32),) * 2,
        grid=(seq // ts,),
        in_specs=[pl.BlockSpec((ts, 1), lambda i: (i, 0)),
                  pl.BlockSpec((1, half), lambda i: (0, 0))],
        out_specs=(pl.BlockSpec((ts, QK_DIM), lambda i: (i, 0)),) * 2,
        compiler_params=_params(),
        name="rope",
    )(positions.reshape(seq, 1), inv_freq.reshape(1, half))


def _ret_kernel(gam_ref, x_ref, vec_ref, cos_ref, sin_ref, wq_ref, wk_ref, wv_ref, wg_ref,
                decay_ref, xi_ref, zeta_ref, gn_ref, wo_ref,
                o_ref,
                r_s, h_s, q_s, k_s, v_s, g_s, z_s, *, tm, chunk):
    @pl.when(pl.program_id(0) == 0)
    def _():
        r_s[...] = jnp.zeros_like(r_s)

    x = x_ref[...]
    h_s[...] = _norm_mod(x, vec_ref).astype(BF16)

    def rope(t):
        return t * cos_ref[...] + pltpu.roll(t, QK_DIM // 2, 1) * sin_ref[...]

    def project(h, carry):
        hb = h_s[...]
        q_s[h] = rope(_dot(hb, wq_ref[h])).astype(BF16)
        k_s[h] = (rope(_dot(hb, wk_ref[h])) * (QK_DIM ** -0.5)).astype(BF16)
        v_s[h] = _dot(hb, wv_ref[h]).astype(BF16)
        g = _dot(hb, wg_ref[h])
        g_s[h] = (g * _sigmoid(g)).astype(BF16)
        return carry

    lax.fori_loop(0, HEADS, project, 0)

    for c in range(tm // chunk):
        rows = pl.ds(c * chunk, chunk)
        for h in range(HEADS):
            qc = q_s[h, rows, :]
            kc = k_s[h, rows, :]
            vc = v_s[h, rows, :]
            s = _dot_nt(qc, kc) * decay_ref[h]
            y = _dot(s.astype(BF16), vc) + _dot(qc, r_s[h].astype(BF16)) * xi_ref[h]
            kz = (kc.astype(F32) * zeta_ref[h]).astype(BF16)
            r_s[h] = r_s[h] * gam_ref[h] + _dot_tn(kz, vc)
            z = _head_norm(y, gn_ref[h]) * g_s[h, rows, :].astype(F32)
            z_s[h, rows, :] = z.astype(BF16)

    out = _dot(z_s[0], wo_ref[0])
    for h in range(1, HEADS):
        out = out + _dot(z_s[h], wo_ref[h])
    o_ref[...] = x + vec_ref[3:4, :] * out


def _retention_layer(x, vecs, cos2, sin2, w_in, gn_g, w_out, *, tm=ROW_BLOCK, chunk=CHUNK):
    seq, d = x.shape
    hq = HEADS * QK_DIM
    hv = HEADS * V_DIM

    def heads(w, width):
        return w.reshape(d, HEADS, width).transpose(1, 0, 2).astype(BF16)

    wq = heads(w_in[:, :hq], QK_DIM)
    wk = heads(w_in[:, hq:2 * hq], QK_DIM)
    wv = heads(w_in[:, 2 * hq:2 * hq + hv], V_DIM)
    wg = heads(w_in[:, 2 * hq + hv:], V_DIM)
    wo = w_out.reshape(HEADS, V_DIM, d).astype(BF16)
    gn = gn_g.reshape(HEADS, 1, V_DIM)

    log_gamma = jnp.log(1.0 - 2.0 ** (-5.0 - jnp.arange(HEADS, dtype=F32)))
    idx = jnp.arange(chunk, dtype=F32)
    rel = idx[:, None] - idx[None, :]
    decay = jnp.where(rel >= 0, jnp.exp(rel[None] * log_gamma[:, None, None]), 0.0)
    xi = jnp.exp((idx + 1.0)[None] * log_gamma[:, None])
    zeta = jnp.exp((chunk - 1.0 - idx)[None] * log_gamma[:, None])
    gamma_c = jnp.exp(chunk * log_gamma)
    xi_b = jnp.broadcast_to(xi[:, :, None], (HEADS, chunk, V_DIM))
    zeta_b = jnp.broadcast_to(zeta[:, :, None], (HEADS, chunk, QK_DIM))

    row = lambda w: pl.BlockSpec((tm, w), lambda i: (i, 0))
    return pl.pallas_call(
        functools.partial(_ret_kernel, tm=tm, chunk=chunk),
        out_shape=jax.ShapeDtypeStruct((seq, d), F32),
        grid=(seq // tm,),
        in_specs=[pl.BlockSpec(memory_space=pltpu.SMEM),
                  row(d), _resident((SUBLANES, d)), row(QK_DIM), row(QK_DIM),
                  _resident(wq.shape), _resident(wk.shape), _resident(wv.shape), _resident(wg.shape),
                  _resident(decay.shape), _resident(xi_b.shape), _resident(zeta_b.shape),
                  _resident(gn.shape), _resident(wo.shape)],
        out_specs=row(d),
        scratch_shapes=[pltpu.VMEM((HEADS, QK_DIM, V_DIM), F32),
                        pltpu.VMEM((tm, d), BF16),
                        pltpu.VMEM((HEADS, tm, QK_DIM), BF16),
                        pltpu.VMEM((HEADS, tm, QK_DIM), BF16),
                        pltpu.VMEM((HEADS, tm, V_DIM), BF16),
                        pltpu.VMEM((HEADS, tm, V_DIM), BF16),
                        pltpu.VMEM((HEADS, tm, V_DIM), BF16)],
        compiler_params=_params(),
        name="ret",
    )(gamma_c, x, vecs, cos2, sin2, wq, wk, wv, wg, decay, xi_b, zeta_b, gn, wo)


def _carry_history(pre_s, tm):
    lead = (slice(None),) * (len(pre_s.shape) - 2)

    @pl.when(pl.program_id(0) == 0)
    def _():
        pre_s[lead + (slice(0, SUBLANES),)] = jnp.zeros(
            pre_s.shape[:-2] + (SUBLANES, pre_s.shape[-1]), F32)

    @pl.when(pl.program_id(0) > 0)
    def _():
        pre_s[lead + (slice(0, SUBLANES),)] = pre_s[lead + (slice(tm, tm + SUBLANES),)]


PHASES = 4


def _phase_rows(r, tm):
    return pl.ds(r, tm // PHASES, stride=PHASES)


def _store_lane_groups(dst_s, g0, p):
    for t in range(p.shape[1] // LANES):
        dst_s[g0 + t, SUBLANES:, :] = p[:, t * LANES:(t + 1) * LANES]


def _causal_conv_phase(pre, cw, taps, tm, r):
    first = SUBLANES - (taps - 1) + r
    out = cw[taps:taps + 1, :] + pre[_phase_rows(first, tm), :] * cw[0:1, :]
    for j in range(1, taps):
        out = out + pre[_phase_rows(first + j, tm), :] * cw[j:j + 1, :]
    return out


def _ffn_kernel(x_ref, vec_ref, wu_ref, cw_ref, wd_ref, fin_ref,
                o_ref,
                h_s, pre_s, act_s, y_s, *, tm, nf, final):
    _carry_history(pre_s, tm)
    x = x_ref[...]
    h_s[...] = _norm_mod(x, vec_ref).astype(BF16)
    n8 = tm // PHASES
    d = x.shape[1]

    for j in range(D_FF // nf):
        for col0 in (j * nf, D_FF + j * nf):
            _store_lane_groups(pre_s, col0 // LANES, _dot(h_s[...], wu_ref[:, pl.ds(col0, nf)]))
        for t in range(nf // LANES):
            la = pl.ds(j * nf + t * LANES, LANES)
            lb = pl.ds(D_FF + j * nf + t * LANES, LANES)
            ga = (j * nf) // LANES + t
            gb = (D_FF + j * nf) // LANES + t
            for r in range(PHASES):
                a = _causal_conv_phase(pre_s.at[ga], cw_ref[:, la], FFN_CONV, tm, r)
                b = _causal_conv_phase(pre_s.at[gb], cw_ref[:, lb], FFN_CONV, tm, r)
                act_s[r * n8:(r + 1) * n8, la] = (a * _sigmoid(a) * b).astype(BF16)

    y = _dot(act_s[...], wd_ref[...])
    for g in range(d // LANES):
        for r in range(PHASES):
            y_s[g, _phase_rows(r, tm), :] = y[r * n8:(r + 1) * n8, g * LANES:(g + 1) * LANES]
    y = x + vec_ref[3:4, :] * jnp.concatenate([y_s[g] for g in range(d // LANES)], axis=1)
    if final:
        ms = jnp.mean(y * y, axis=-1, keepdims=True)
        y = y * lax.rsqrt(ms + EPS) * fin_ref[...]
    o_ref[...] = y


def _conv_ffn_layer(x, vecs, w_up, conv_w, conv_b, w_down, final_g, *, final, tm=ROW_BLOCK, nf=FFN_COLS):
    seq, d = x.shape
    k = conv_w.shape[0]
    cw = jnp.concatenate([conv_w, conv_b[None, :], jnp.zeros((SUBLANES - k - 1, 2 * D_FF), F32)], axis=0)
    row = pl.BlockSpec((tm, d), lambda i: (i, 0))
    return pl.pallas_call(
        functools.partial(_ffn_kernel, tm=tm, nf=nf, final=final),
        out_shape=jax.ShapeDtypeStruct((seq, d), F32),
        grid=(seq // tm,),
        in_specs=[row, _resident((SUBLANES, d)), _resident(w_up.shape), _resident(cw.shape),
                  _resident(w_down.shape), _resident((1, d))],
        out_specs=row,
        scratch_shapes=[pltpu.VMEM((tm, d), BF16),
                        pltpu.VMEM((2 * D_FF // LANES, SUBLANES + tm, LANES), F32),
                        pltpu.VMEM((tm, D_FF), BF16),
                        pltpu.VMEM((d // LANES, tm, LANES), F32)],
        compiler_params=_params(),
        name="ffn_final" if final else "ffn",
    )(x, vecs, w_up.astype(BF16), cw, w_down.astype(BF16), final_g.reshape(1, d))


def _log_sigmoid(x):
    return jnp.minimum(x, 0.0) - jnp.log1p(jnp.exp(-jnp.abs(x)))


def _dot_f32_lhs(x, w):
    hi = x.astype(BF16)
    r1 = x - hi.astype(F32)
    mid = r1.astype(BF16)
    lo = (r1 - mid.astype(F32)).astype(BF16)
    return _dot(hi, w) + _dot(mid, w) + _dot(lo, w)


def _mlstm_kernel(x_ref, vec_ref, wqk_ref, cw_ref, wv_ref, wo_ref, wg_ref, bg_ref, tri_ref,
                  neg_ref, gn_ref, wout_ref,
                  o_ref,
                  cm_s, n_s, m_s, pre_s, act_s, h_s, q_s, k_s, kt_s, v_s, og_s, z_s, col_s, row_s, *, tm, chunk):
    @pl.when(pl.program_id(0) == 0)
    def _():
        cm_s[...] = jnp.zeros_like(cm_s)
        n_s[...] = jnp.zeros_like(n_s)
        m_s[...] = jnp.zeros_like(m_s)

    _carry_history(pre_s, tm)
    x = x_ref[...]
    h_s[...] = _norm_mod(x, vec_ref).astype(BF16)
    groups = QK_DIM // LANES

    def pre_project(h):
        for slot in (h, HEADS + h):
            _store_lane_groups(pre_s, slot * groups, _dot(h_s[...], wqk_ref[slot]))

    def conv_silu(slot, stage):
        for t in range(groups):
            lanes = pl.ds(t * LANES, LANES)
            for r in range(PHASES):
                c = _causal_conv_phase(pre_s.at[slot * groups + t], cw_ref[slot, :, lanes], MLSTM_CONV, tm, r)
                act_s[stage + t, _phase_rows(r, tm), :] = c * _sigmoid(c)

    def project(h, more):
        if more:
            pre_project(h + 1)
        conv_silu(h, 0)
        conv_silu(HEADS + h, groups)
        q_s[h] = jnp.concatenate([act_s[t] for t in range(groups)], axis=1).astype(BF16)
        k = jnp.concatenate([act_s[groups + t] for t in range(groups)], axis=1) * (QK_DIM ** -0.5)
        k_s[h] = k.astype(BF16)
        kt_s[h] = k.T.astype(BF16)
        hb = h_s[...]
        v_s[h] = _dot(hb, wv_ref[h]).astype(BF16)
        og_s[h] = _sigmoid(_dot(hb, wo_ref[h])).astype(BF16)

    pre_project(0)

    gr = _dot_nt(wg_ref[...], h_s[...]) + bg_ref[...]
    fr = _dot_f32_lhs(_log_sigmoid(gr), tri_ref[...])
    gates = jnp.where(lax.broadcasted_iota(jnp.int32, gr.shape, 0) < HEADS, gr, fr)
    row_s[...] = gates
    col_s[...] = jnp.concatenate([gates, jnp.zeros((LANES - gates.shape[0], tm), F32)], axis=0).T

    def project_more(h, carry):
        project(h, True)
        return carry

    lax.fori_loop(0, HEADS - 1, project_more, 0)
    project(HEADS - 1, False)

    for c in range(tm // chunk):
        r0 = c * chunk
        rows = pl.ds(r0, chunk)
        for h in range(HEADS):
            f = HEADS + h
            gcol = col_s[rows, f:f + 1]
            grow = row_s[f:f + 1, rows]
            irow = row_s[h:h + 1, rows]
            g_last = col_s[r0 + chunk - 1:r0 + chunk, f:f + 1]
            m = m_s[h, 0:1, 0:1]
            brow = irow - grow
            a = gcol + m
            dlog = (gcol + brow) + neg_ref[...]
            m_row = jnp.maximum(a, jnp.max(dlog, axis=-1, keepdims=True))
            w_intra = jnp.exp(dlog - m_row)
            w_inter = jnp.exp(a - m_row)
            qc = q_s[h, rows, :]
            kc = k_s[h, rows, :]
            vc = v_s[h, rows, :]
            s = _dot_nt(qc, kc) * w_intra
            num = _dot(s.astype(BF16), vc) + w_inter * _dot(qc, cm_s[h].astype(BF16))
            qn = jnp.sum(qc.astype(F32) * n_s[h, 0:1, :], axis=-1, keepdims=True)
            den = jnp.sum(s, axis=-1, keepdims=True) + w_inter * qn
            h_til = num * (1.0 / jnp.maximum(jnp.abs(den), jnp.exp(-m_row)))
            m_new = jnp.maximum(g_last + m, jnp.max(g_last + brow, axis=-1, keepdims=True))
            w_state = jnp.exp((g_last + brow) - m_new)
            keep = jnp.exp(g_last + m - m_new)
            kwt = (kt_s[h, :, rows].astype(F32) * w_state).astype(BF16)
            cm_s[h] = keep * cm_s[h] + _dot(kwt, vc)
            w8 = jnp.broadcast_to(w_state, (2 * SUBLANES, chunk)).astype(BF16)
            n_new = keep * n_s[h, 0:1, :] + _dot(w8, kc)[0:1, :]
            n_s[h] = jnp.broadcast_to(n_new, (SUBLANES, QK_DIM))
            m_s[h] = jnp.broadcast_to(m_new, (SUBLANES, LANES))
            y = og_s[h, rows, :].astype(F32) * h_til
            z_s[h, rows, :] = _head_norm(y, gn_ref[h]).astype(BF16)

    out = _dot(z_s[0], wout_ref[0])
    for h in range(1, HEADS):
        out = out + _dot(z_s[h], wout_ref[h])
    o_ref[...] = x + vec_ref[3:4, :] * out


def _mlstm_layer(x, vecs, w_in, b_gate, conv_w, conv_b, gn_g, w_out, *, tm=ROW_BLOCK, chunk=CHUNK):
    seq, d = x.shape
    hq = HEADS * QK_DIM
    hv = HEADS * V_DIM
    n_gate = 2 * HEADS

    def heads(w, n, width):
        return w.reshape(d, n, width).transpose(1, 0, 2).astype(BF16)

    wqk = heads(w_in[:, :2 * hq], 2 * HEADS, QK_DIM)
    wv = heads(w_in[:, 2 * hq:2 * hq + hv], HEADS, V_DIM)
    wo = heads(w_in[:, 2 * hq + hv:2 * hq + 2 * hv], HEADS, V_DIM)
    w_gate = w_in[:, 2 * hq + 2 * hv:]
    wg = jnp.pad(w_gate.T, ((0, 2 * SUBLANES - n_gate), (0, 0))).astype(BF16)
    bg = jnp.pad(b_gate, (0, 2 * SUBLANES - n_gate)).reshape(2 * SUBLANES, 1)
    k = conv_w.shape[0]
    cw = jnp.concatenate([conv_w, conv_b[None, :], jnp.zeros((SUBLANES - k - 1, 2 * hq), F32)], axis=0)
    cw = cw.reshape(SUBLANES, 2 * HEADS, QK_DIM).transpose(1, 0, 2)
    idx = jnp.arange(chunk)
    neg = jnp.where(idx[:, None] >= idx[None, :], 0.0, -jnp.inf).astype(F32)
    pos = jnp.arange(tm)
    tri = ((pos[:, None] <= pos[None, :]) & (pos[:, None] // chunk == pos[None, :] // chunk)).astype(BF16)
    gn = gn_g.reshape(HEADS, 1, V_DIM)
    wout = w_out.reshape(HEADS, V_DIM, d).astype(BF16)

    row = pl.BlockSpec((tm, d), lambda i: (i, 0))
    operands = (x, vecs, wqk, cw, wv, wo, wg, bg, tri, neg, gn, wout)
    return pl.pallas_call(
        functools.partial(_mlstm_kernel, tm=tm, chunk=chunk),
        out_shape=jax.ShapeDtypeStruct((seq, d), F32),
        grid=(seq // tm,),
        in_specs=[row, _resident((SUBLANES, d))] + [_resident(a.shape) for a in operands[2:]],
        out_specs=row,
        scratch_shapes=[pltpu.VMEM((HEADS, QK_DIM, V_DIM), F32),
                        pltpu.VMEM((HEADS, SUBLANES, QK_DIM), F32),
                        pltpu.VMEM((HEADS, SUBLANES, LANES), F32),
                        pltpu.VMEM((2 * hq // LANES, SUBLANES + tm, LANES), F32),
                        pltpu.VMEM((2 * QK_DIM // LANES, tm, LANES), F32),
                        pltpu.VMEM((tm, d), BF16),
                        pltpu.VMEM((HEADS, tm, QK_DIM), BF16),
                        pltpu.VMEM((HEADS, tm, QK_DIM), BF16),
                        pltpu.VMEM((HEADS, QK_DIM, tm), BF16),
                        pltpu.VMEM((HEADS, tm, V_DIM), BF16),
                        pltpu.VMEM((HEADS, tm, V_DIM), BF16),
                        pltpu.VMEM((HEADS, tm, V_DIM), BF16),
                        pltpu.VMEM((tm, LANES), F32),
                        pltpu.VMEM((2 * SUBLANES, tm), F32)],
        compiler_params=_params(),
        name="mlstm",
    )(*operands)


def _vecs(norm_g, mod_row, which):
    d = norm_g.shape[0]
    sh, sc, gt = (mod_row[(3 * which + j) * d:(3 * which + j + 1) * d] for j in range(3))
    return jnp.concatenate([jnp.stack([norm_g, sc, sh, gt]), jnp.zeros((SUBLANES - 4, d), F32)], axis=0)


def kernel(x, c, positions, ada_w, ada_b, norm_tok_g, norm_ffn_g, ret_w_in, ret_gn_g, ret_w_out, ml_w_in, ml_b_gate, ml_conv_w, ml_conv_b, ml_gn_g, ml_w_out, ffn_w_up, ffn_conv_w, ffn_conv_b, ffn_w_down, final_g):
    batch, seq, d = x.shape
    assert batch == 1 and d == D_MODEL and ada_w.shape[0] == DEPTH == 2
    mod = _modulation(c, ada_w, ada_b)[:, 0, :]
    cos2, sin2 = _rope_tables(positions, seq)
    h = x.reshape(seq, d)
    h = _retention_layer(h, _vecs(norm_tok_g[0], mod[0], 0), cos2, sin2,
                         ret_w_in[0], ret_gn_g[0], ret_w_out[0])
    h = _conv_ffn_layer(h, _vecs(norm_ffn_g[0], mod[0], 1), ffn_w_up[0], ffn_conv_w[0], ffn_conv_b[0],
                        ffn_w_down[0], final_g, final=False)
    h = _mlstm_layer(h, _vecs(norm_tok_g[1], mod[1], 0), ml_w_in[0], ml_b_gate[0], ml_conv_w[0],
                     ml_conv_b[0], ml_gn_g[0], ml_w_out[0])
    h = _conv_ffn_layer(h, _vecs(norm_ffn_g[1], mod[1], 1), ffn_w_up[1], ffn_conv_w[1], ffn_conv_b[1],
                        ffn_w_down[1], final_g, final=True)
    return h.reshape(batch, seq, d)
```
